```python
import jax, jax.numpy as jnp
from jax import lax
import numpy as np

D_MODEL = 1024
BATCH = 16
SEQ = 2048
DEPTH = 4

HEAD_DIM = 64
A_HEADS = 8
A_KV_HEADS = 2
B_GROUPS = ((128, 1), (512, 4), (2048, 16))
B_HEADS_PER_GROUP = 4
B_HEADS = len(B_GROUPS) * B_HEADS_PER_GROUP
GRID_W = 64
ROPE_THETA = 10000.0
Q_BLOCK = 128
N_EXPERTS = 16
N_GROUPS = 4
EXPERTS_PER_GROUP = N_EXPERTS // N_GROUPS
TOP_K = 2
D_EXPERT = 256
LN_EPS = 1e-5
RMS_EPS = 1e-6
DN_ALPHA = (2.0 * DEPTH) ** 0.25
DN_BETA = (8.0 * DEPTH) ** -0.25

A_Q = A_HEADS * HEAD_DIM
A_KV = A_KV_HEADS * HEAD_DIM
B_W = B_HEADS * HEAD_DIM
B_OUT = B_HEADS_PER_GROUP * HEAD_DIM
N_IN = A_Q + 2 * A_KV + 3 * B_W + 2 * D_MODEL
IN_SPLITS = (A_Q, A_Q + A_KV, A_Q + 2 * A_KV, A_Q + 2 * A_KV + B_W,
             A_Q + 2 * A_KV + 2 * B_W, A_Q + 2 * A_KV + 3 * B_W)

kernel_name = "hybrid_gqa_dilated_moe_encoder"


def _layer_norm(x, g=None, b=None):
    xf = x.astype(jnp.float32)
    mu = jnp.mean(xf, -1, keepdims=True)
    var = jnp.mean(jnp.square(xf - mu), -1, keepdims=True)
    y = (xf - mu) * lax.rsqrt(var + LN_EPS)
    if g is not None:
        y = y * g.astype(jnp.float32) + b.astype(jnp.float32)
    return y.astype(x.dtype)


def _rms_norm(x, g):
    xf = x.astype(jnp.float32)
    y = xf * lax.rsqrt(jnp.mean(jnp.square(xf), -1, keepdims=True) + RMS_EPS)
    return (y * g.astype(jnp.float32)).astype(x.dtype)


def _axial_rope_tables(seq):
    rows = seq // GRID_W
    row = jnp.repeat(jnp.arange(rows), GRID_W).astype(jnp.float32)
    col = jnp.tile(jnp.arange(GRID_W), rows).astype(jnp.float32)
    axis_dim = HEAD_DIM // 2
    inv = ROPE_THETA ** (-jnp.arange(0, axis_dim, 2, dtype=jnp.float32) / axis_dim)
    ang = jnp.concatenate([row[:, None] * inv, col[:, None] * inv], -1)
    return jnp.cos(ang), jnp.sin(ang)


def _apply_rope(x, cos, sin):
    xf = x.astype(jnp.float32).reshape(x.shape[:-1] + (HEAD_DIM // 2, 2))
    x0, x1 = xf[..., 0], xf[..., 1]
    c = cos[None, :, None, :]
    s = sin[None, :, None, :]
    out = jnp.stack([x0 * c - x1 * s, x0 * s + x1 * c], -1).reshape(x.shape)
    return out.astype(x.dtype)


def _alibi_slopes(n):
    return jnp.exp2(-8.0 * jnp.arange(1, n + 1, dtype=jnp.float32) / n)


def _global_gqa(q, k, v):
    b, s = q.shape[0], q.shape[1]
    rep = A_HEADS // A_KV_HEADS
    nblk = s // Q_BLOCK
    scale = HEAD_DIM ** -0.5
    qb = q.reshape(b, nblk, Q_BLOCK, A_KV_HEADS, rep, HEAD_DIM).transpose(1, 0, 2, 3, 4, 5)

    def block(qi):
        sc = jnp.einsum('bqgrd,bkgd->bgrqk', qi, k).astype(jnp.float32) * scale
        p = jax.nn.softmax(sc, axis=-1)
        return jnp.einsum('bgrqk,bkgd->bqgrd', p.astype(v.dtype), v)

    o = lax.map(block, qb)
    return o.transpose(1, 0, 2, 3, 4, 5).reshape(b, s, A_Q)


def _dilated_group(q, k, v, window, dilation, slopes):
    b, s, h = q.shape[0], q.shape[1], q.shape[2]
    reach = window // (2 * dilation)
    offs = dilation * jnp.arange(-reach, reach + 1)
    nblk = s // Q_BLOCK
    scale = HEAD_DIM ** -0.5
    qb = q.reshape(b, nblk, Q_BLOCK, h, HEAD_DIM).transpose(1, 0, 2, 3, 4)
    starts = jnp.arange(nblk) * Q_BLOCK
    bias = -slopes[:, None] * jnp.abs(offs).astype(jnp.float32)[None, :]

    def block(args):
        qi, t0 = args
        pos = t0 + jnp.arange(Q_BLOCK)[:, None] + offs[None, :]
        valid = (pos >= 0) & (pos < s)
        idx = jnp.clip(pos, 0, s - 1)
        kg = jnp.take(k, idx, axis=1)
        vg = jnp.take(v, idx, axis=1)
        sc = jnp.einsum('bqhd,bqkhd->bhqk', qi, kg).astype(jnp.float32) * scale + bias[None, :, None, :]
        sc = jnp.where(valid[None, None], sc, -jnp.inf)
        lse = jax.nn.logsumexp(sc, axis=-1)
        p = jnp.exp(sc - lse[..., None])
        o = jnp.einsum('bhqk,bqkhd->bqhd', p.astype(v.dtype), vg)
        return o, lse.transpose(0, 2, 1)

    o, lse = lax.map(block, (qb, starts))
    o = o.transpose(1, 0, 2, 3, 4).reshape(b, s, h, HEAD_DIM)
    lse = lse.transpose(1, 0, 2, 3).reshape(b, s, h)
    return o, lse


def _mixer(h, w_in, qn_g, kn_g, w_pa, w_pb, w_o, cos, sin, slopes):
    b, s, _ = h.shape
    proj = h @ w_in
    qa, ka, va, qd, kd, vd, gates = jnp.split(proj, IN_SPLITS, axis=-1)
    qa = _apply_rope(_rms_norm(qa.reshape(b, s, A_HEADS, HEAD_DIM), qn_g), cos, sin)
    ka = _apply_rope(_rms_norm(ka.reshape(b, s, A_KV_HEADS, HEAD_DIM), kn_g), cos, sin)
    va = va.reshape(b, s, A_KV_HEADS, HEAD_DIM)
    out_a = _global_gqa(qa, ka, va)
    qd = qd.reshape(b, s, B_HEADS, HEAD_DIM)
    kd = kd.reshape(b, s, B_HEADS, HEAD_DIM)
    vd = vd.reshape(b, s, B_HEADS, HEAD_DIM)
    outs, lses = [], []
    for g, (window, dilation) in enumerate(B_GROUPS):
        sl = slice(g * B_HEADS_PER_GROUP, (g + 1) * B_HEADS_PER_GROUP)
        o, l = _dilated_group(qd[:, :, sl], kd[:, :, sl], vd[:, :, sl], window, dilation, slopes[sl])
        outs.append(o)
        lses.append(l)
    wts = jax.nn.softmax(jnp.stack(lses, 0), axis=0)
    out_b = jnp.sum(wts[..., None].astype(h.dtype) * jnp.stack(outs, 0), 0).reshape(b, s, B_OUT)
    g_a, g_b = jnp.split(jax.nn.sigmoid(gates), 2, axis=-1)
    merged = g_a * (out_a @ w_pa) + g_b * (out_b @ w_pb)
    return merged @ w_o


def _moe(h, w_router, router_bias, w_gate, w_up, w_down):
    b, s, d = h.shape
    t = h.reshape(b * s, d)
    scores = jax.nn.sigmoid((t @ w_router).astype(jnp.float32))
    sel = scores + router_bias.astype(jnp.float32)
    grp_score = jnp.sum(lax.top_k(sel.reshape(-1, N_GROUPS, EXPERTS_PER_GROUP), TOP_K)[0], -1)
    best = jnp.argmax(grp_score, axis=-1)
    in_group = (jnp.arange(N_EXPERTS) // EXPERTS_PER_GROUP)[None, :] == best[:, None]
    _, idx = lax.top_k(jnp.where(in_group, sel, -jnp.inf), TOP_K)
    top_s = jnp.take_along_axis(scores, idx, axis=-1)
    gate = top_s / jnp.sum(top_s, -1, keepdims=True)
    combine = jnp.sum(jax.nn.one_hot(idx, N_EXPERTS, dtype=jnp.float32) * gate[..., None], 1)
    hg = jnp.einsum('td,edf->tef', t, w_gate)
    hu = jnp.einsum('td,edf->tef', t, w_up)
    act = jax.nn.silu(hg) * hu * combine[..., None].astype(t.dtype)
    y = jnp.einsum('tef,efd->td', act, w_down)
    return y.reshape(b, s, d)


def setup_inputs(seed: int = 0) -> dict:
    key = jax.random.key(seed)
    ks = jax.random.split(key, 24)
    f32 = jnp.float32
    n = lambda k, shape, sc: jax.random.normal(k, shape, f32) * sc
    d = D_MODEL
    return {
        "x": n(ks[0], (BATCH, SEQ, d), 1.0),
        "c": n(ks[1], (BATCH, d), 1.0),
        "w_ada": n(ks[2], (DEPTH, d, 6 * d), 0.5 * d ** -0.5),
        "b_ada": n(ks[3], (DEPTH, 6 * d), 0.02),
        "w_in": n(ks[4], (DEPTH, d, N_IN), d ** -0.5),
        "q_norm_g": 1.0 + n(ks[5], (DEPTH, HEAD_DIM), 0.02),
        "k_norm_g": 1.0 + n(ks[6], (DEPTH, HEAD_DIM), 0.02),
        "w_branch_a": n(ks[7], (DEPTH, A_Q, d), A_Q ** -0.5),
        "w_branch_b": n(ks[8], (DEPTH, B_OUT, d), B_OUT ** -0.5),
        "w_out": n(ks[9], (DEPTH, d, d), DN_BETA * d ** -0.5),
        "ln1_g": 1.0 + n(ks[10], (DEPTH, d), 0.02),
        "ln1_b": n(ks[11], (DEPTH, d), 0.02),
        "w_router": n(ks[12], (d, N_EXPERTS), d ** -0.5),
        "router_bias": n(ks[13], (N_EXPERTS,), 0.01),
        "w_exp_gate": n(ks[14], (DEPTH, N_EXPERTS, d, D_EXPERT), d ** -0.5),
        "w_exp_up": n(ks[15], (DEPTH, N_EXPERTS, d, D_EXPERT), d ** -0.5),
        "w_exp_down": n(ks[16], (DEPTH, N_EXPERTS, D_EXPERT, d), DN_BETA * D_EXPERT ** -0.5),
        "ln2_g": 1.0 + n(ks[17], (DEPTH, d), 0.02),
        "ln2_b": n(ks[18], (DEPTH, d), 0.02),
    }


def reference(x, c, w_ada, b_ada, w_in, q_norm_g, k_norm_g, w_branch_a, w_branch_b, w_out,
              ln1_g, ln1_b, w_router, router_bias, w_exp_gate, w_exp_up, w_exp_down, ln2_g, ln2_b):
    s = x.shape[1]
    cos, sin = _axial_rope_tables(s)
    slopes = _alibi_slopes(B_HEADS)
    cond = jax.nn.silu(c)
    for l in range(DEPTH):
        mod = cond @ w_ada[l] + b_ada[l]
        sh1, sc1, g1, sh2, sc2, g2 = [m[:, None, :] for m in jnp.split(mod, 6, axis=-1)]
        h = _layer_norm(x) * (1.0 + sc1) + sh1
        mix = _mixer(h, w_in[l], q_norm_g[l], k_norm_g[l], w_branch_a[l], w_branch_b[l],
                     w_out[l], cos, sin, slopes)
        x = _layer_norm(DN_ALPHA * x + g1 * mix, ln1_g[l], ln1_b[l])
        h = _layer_norm(x) * (1.0 + sc2) + sh2
        ffn = _moe(h, w_router, router_bias, w_exp_gate[l], w_exp_up[l], w_exp_down[l])
        x = _layer_norm(DN_ALPHA * x + g2 * ffn, ln2_g[l], ln2_b[l])
    return x
```

```python
import functools
import math

import jax
import jax.numpy as jnp
from jax import lax
from jax.experimental import pallas as pl
from jax.experimental.pallas import tpu as pltpu

F32 = jnp.float32
BF16 = jnp.bfloat16

D_MODEL = 1024
HEAD_DIM = 64
A_HEADS = 8
A_KV_HEADS = 2
B_GROUPS = ((128, 1), (512, 4), (2048, 16))
B_HEADS_PER_GROUP = 4
B_HEADS = len(B_GROUPS) * B_HEADS_PER_GROUP
GRID_W = 64
ROPE_THETA = 10000.0
Q_BLOCK = 128
N_EXPERTS = 16
N_GROUPS = 4
EXPERTS_PER_GROUP = N_EXPERTS // N_GROUPS
D_EXPERT = 256
LN_EPS = 1e-5
RMS_EPS = 1e-6

A_Q = A_HEADS * HEAD_DIM
A_KV = A_KV_HEADS * HEAD_DIM
B_W = B_HEADS * HEAD_DIM
B_OUT = B_HEADS_PER_GROUP * HEAD_DIM
REACH = 64
assert all(w // (2 * d) == REACH for w, d in B_GROUPS)

LANES = 128
VMEM_LIMIT = 56 * 1024 * 1024

KDUP = 2 * A_KV
C_QA = 0
C_KA = C_QA + A_Q
C_VA = C_KA + KDUP
C_QD = C_VA + KDUP
C_KD = C_QD + B_W
C_VD = C_KD + B_W
C_GT = C_VD + B_W
N_PROJ = C_GT + 2 * D_MODEL


def _ln(x):
    mu = jnp.mean(x, -1, keepdims=True)
    xc = x - mu
    var = jnp.mean(xc * xc, -1, keepdims=True)
    return xc * lax.rsqrt(var + LN_EPS)


def _nt_dot(a, b, **kw):
    return lax.dot_general(a, b, (((1,), (1,)), ((), ())), preferred_element_type=F32, **kw)


def _params(*sem):
    return pltpu.CompilerParams(dimension_semantics=sem, vmem_limit_bytes=VMEM_LIMIT)


def _ada_kernel(c_ref, w_ref, b_ref, o_ref):
    c = c_ref[...]
    cond = c / (1.0 + jnp.exp(-c))
    o_ref[...] = jnp.dot(cond, w_ref[...], precision=lax.Precision.HIGHEST,
                         preferred_element_type=F32) + b_ref[...]


def _ada_call(c, w_ada, b_ada):
    depth, d, n = w_ada.shape
    bsz = c.shape[0]
    tn = 1024
    return pl.pallas_call(
        _ada_kernel,
        out_shape=jax.ShapeDtypeStruct((depth, bsz, n), F32),
        grid=(depth, n // tn),
        in_specs=[
            pl.BlockSpec((bsz, d), lambda l, j: (0, 0)),
            pl.BlockSpec((None, d, tn), lambda l, j: (l, 0, j)),
            pl.BlockSpec((None, 1, tn), lambda l, j: (l, 0, j)),
        ],
        out_specs=pl.BlockSpec((None, bsz, tn), lambda l, j: (l, 0, j)),
        compiler_params=_params("arbitrary", "arbitrary"),
        name="ada_mod",
    )(c, w_ada, b_ada.reshape(depth, 1, n))


def _inproj_kernel(x_ref, sc_ref, sh_ref, w_ref, gq_ref, gk_ref, cos_ref, sin_ref, bd_ref,
                   qa_ref, ka_ref, va_ref, qd_ref, kd_ref, vd_ref, gt_ref):
    tm = x_ref.shape[0]
    h = (_ln(x_ref[...]) * (1.0 + sc_ref[...]) + sh_ref[...]).astype(BF16)

    def proj(a, b):
        return jnp.dot(h, w_ref[:, a:b], preferred_element_type=F32)

    lane = lax.broadcasted_iota(jnp.int32, (tm, LANES), 1)
    even = (lane & 1) == 0
    cos = cos_ref[...]
    sin = sin_ref[...]
    bd = bd_ref[...]

    def norm_rope(y, g):
        sq = y * y
        hi = sq.astype(BF16)
        lo = (sq - hi.astype(F32)).astype(BF16)
        ss = jnp.dot(hi, bd, preferred_element_type=F32) + jnp.dot(lo, bd, preferred_element_type=F32)
        yn = y * lax.rsqrt(ss * (1.0 / HEAD_DIM) + RMS_EPS) * g
        partner = jnp.where(even, pltpu.roll(yn, LANES - 1, 1), pltpu.roll(yn, 1, 1))
        return yn * cos + partner * sin

    qa = proj(C_QA, C_KA)
    gq = gq_ref[...]
    for p in range(A_Q // LANES):
        sl = slice(p * LANES, (p + 1) * LANES)
        qa_ref[:, sl] = norm_rope(qa[:, sl], gq).astype(BF16)
    ka = proj(C_KA, C_VA)
    gk = gk_ref[...]
    for p in range(KDUP // LANES):
        sl = slice(p * LANES, (p + 1) * LANES)
        ka_ref[:, sl] = norm_rope(ka[:, sl], gk).astype(BF16)
    va_ref[...] = proj(C_VA, C_QD).astype(BF16)
    qd_ref[...] = (proj(C_QD, C_KD) * (HEAD_DIM ** -0.5)).astype(BF16)
    kd_ref[...] = proj(C_KD, C_VD).astype(BF16)
    vd_ref[...] = proj(C_VD, C_GT).astype(BF16)
    for p in range(2):
        a = C_GT + p * D_MODEL
        g = proj(a, a + D_MODEL)
        gt_ref[:, p * D_MODEL:(p + 1) * D_MODEL] = (1.0 / (1.0 + jnp.exp(-g))).astype(BF16)


def _inproj_call(x, sc, sh, w_ext, l, gq, gk, cos_t, sin_t, bd, tm):
    bsz, s, d = x.shape
    row = lambda n: pl.BlockSpec((None, tm, n), lambda b, i: (b, i, 0))
    vec = lambda n: pl.BlockSpec((None, 1, n), lambda b, i: (b, 0, 0))
    out = lambda n: jax.ShapeDtypeStruct((bsz, s, n), BF16)
    return pl.pallas_call(
        _inproj_kernel,
        out_shape=[out(A_Q), out(KDUP), out(KDUP), out(B_W), out(B_W), out(B_W), out(2 * D_MODEL)],
        grid=(bsz, s // tm),
        in_specs=[
            row(d), vec(d), vec(d),
            pl.BlockSpec((None, d, N_PROJ), lambda b, i: (l, 0, 0)),
            pl.BlockSpec((None, 1, LANES), lambda b, i: (l, 0, 0)),
            pl.BlockSpec((None, 1, LANES), lambda b, i: (l, 0, 0)),
            pl.BlockSpec((tm, LANES), lambda b, i: (i, 0)),
            pl.BlockSpec((tm, LANES), lambda b, i: (i, 0)),
            pl.BlockSpec((LANES, LANES), lambda b, i: (0, 0)),
        ],
        out_specs=[row(A_Q), row(KDUP), row(KDUP), row(B_W), row(B_W), row(B_W), row(2 * D_MODEL)],
        compiler_params=_params("arbitrary", "arbitrary"),
        name="in_proj",
    )(x, sc, sh, w_ext, gq, gk, cos_t, sin_t, bd)


def _gqa_kernel(q_ref, k_ref, v_ref, o_ref):
    lo = lax.broadcasted_iota(jnp.int32, (1, LANES), 1) < HEAD_DIM
    heads_per_slab = LANES // HEAD_DIM
    rep = A_HEADS // A_KV_HEADS
    for p in range(A_Q // LANES):
        g = (p * heads_per_slab) // rep
        qs = q_ref[:, p * LANES:(p + 1) * LANES]
        kk = k_ref[:, g * LANES:(g + 1) * LANES]
        vv = v_ref[:, g * LANES:(g + 1) * LANES]
        outs = []
        for half in range(heads_per_slab):
            qm = jnp.where(lo if half == 0 else jnp.logical_not(lo), qs, jnp.zeros_like(qs))
            s = _nt_dot(qm, kk)
            mx = jnp.max(s, -1, keepdims=True)
            p_un = jnp.exp(s - mx)
            den = jnp.sum(p_un, -1, keepdims=True)
            outs.append(jnp.dot(p_un.astype(BF16), vv, preferred_element_type=F32) / den)
        o_ref[:, p * LANES:(p + 1) * LANES] = jnp.where(lo, outs[0], outs[1]).astype(BF16)


def _gqa_call(qa, ka, va, tq):
    bsz, s, _ = qa.shape
    return pl.pallas_call(
        _gqa_kernel,
        out_shape=jax.ShapeDtypeStruct((bsz, s, A_Q), BF16),
        grid=(bsz, s // tq),
        in_specs=[
            pl.BlockSpec((None, tq, A_Q), lambda b, i: (b, i, 0)),
            pl.BlockSpec((None, s, KDUP), lambda b, i: (b, 0, 0)),
            pl.BlockSpec((None, s, KDUP), lambda b, i: (b, 0, 0)),
        ],
        out_specs=pl.BlockSpec((None, tq, A_Q), lambda b, i: (b, i, 0)),
        compiler_params=_params("arbitrary", "arbitrary"),
        name="gqa_attn",
    )(qa, ka, va)


def _alibi_slope(head):
    return 2.0 ** (-8.0 * (head + 1) / B_HEADS)


def _dilated_tile(q, k, v, qpos0, kpos0, group):
    w = k.shape[0]
    dilation = B_GROUPS[group][1]
    head_id = lax.broadcasted_iota(jnp.int32, (1, B_OUT), 1) // HEAD_DIM
    qi = qpos0 + lax.broadcasted_iota(jnp.int32, (Q_BLOCK, w), 0)
    kj = kpos0 + lax.broadcasted_iota(jnp.int32, (Q_BLOCK, w), 1)
    dist = jnp.abs(qi - kj)
    valid = dist <= REACH
    distf = dist.astype(F32)
    o_acc = jnp.zeros((Q_BLOCK, B_OUT), F32)
    lse_acc = jnp.zeros((Q_BLOCK, B_OUT), F32)
    for h in range(B_HEADS_PER_GROUP):
        hm = head_id == h
        qm = jnp.where(hm, q, jnp.zeros_like(q))
        slope = _alibi_slope(group * B_HEADS_PER_GROUP + h) * dilation
        s = jnp.where(valid, _nt_dot(qm, k) - slope * distf, -jnp.inf)
        mx = jnp.max(s, -1, keepdims=True)
        p_un = jnp.exp(s - mx)
        den = jnp.sum(p_un, -1, keepdims=True)
        o = jnp.dot(p_un.astype(BF16), v, preferred_element_type=F32) / den
        o_acc = jnp.where(hm, o, o_acc)
        lse_acc = jnp.where(hm, mx + jnp.log(den), lse_acc)
    return o_acc, lse_acc


def _dilated_kernel(q0_ref, k0_ref, v0_ref, q1_ref, k1_ref, v1_ref, q2_ref, k2_ref, v2_ref,
                    o0_ref, l0_ref, o1_ref, l1_ref, o2_ref, l2_ref):
    j = pl.program_id(1)
    kwin = 2 * Q_BLOCK

    def banded(q_ref, k_ref, v_ref, o_ref, l_ref, blk, group):
        seq = k_ref.shape[0]
        start = pl.multiple_of(jnp.clip(blk * Q_BLOCK - REACH, 0, seq - kwin), REACH)
        o, lse = _dilated_tile(q_ref[...], k_ref[pl.ds(start, kwin), :], v_ref[pl.ds(start, kwin), :],
                               blk * Q_BLOCK, start, group)
        o_ref[...] = o.astype(BF16)
        l_ref[...] = lse

    banded(q0_ref, k0_ref, v0_ref, o0_ref, l0_ref, j, 0)
    banded(q1_ref, k1_ref, v1_ref, o1_ref, l1_ref, j % 4, 1)
    o, lse = _dilated_tile(q2_ref[...], k2_ref[...], v2_ref[...], 0, 0, 2)
    o2_ref[...] = o.astype(BF16)
    l2_ref[...] = lse


def _dilated_call(qd, kd, vd):
    bsz, s, _ = qd.shape
    d1, d2 = B_GROUPS[1][1], B_GROUPS[2][1]
    s1, s2 = s // d1, s // d2
    nblk = s // Q_BLOCK
    assert nblk == d2 and s2 == Q_BLOCK and s1 // Q_BLOCK == d1
    n_hg = B_W // B_OUT
    v1 = lambda a: a.reshape(bsz, s1, d1 * B_W)
    v2 = lambda a: a.reshape(bsz, s2, d2 * B_W)
    blk = lambda rows, imap: pl.BlockSpec((None, rows, B_OUT), imap)
    g0_q = blk(Q_BLOCK, lambda b, j: (b, j, 0))
    g0_kv = blk(s, lambda b, j: (b, 0, 0))
    g1_q = blk(Q_BLOCK, lambda b, j: (b, j % 4, (j // 4) * n_hg + 1))
    g1_kv = blk(s1, lambda b, j: (b, 0, (j // 4) * n_hg + 1))
    g2 = blk(Q_BLOCK, lambda b, j: (b, 0, j * n_hg + 2))
    o0 = blk(Q_BLOCK, lambda b, j: (b, j, 0))
    o1 = blk(Q_BLOCK, lambda b, j: (b, j % 4, j // 4))
    o2 = blk(Q_BLOCK, lambda b, j: (b, 0, j))
    shp = lambda rows, cols, dt: jax.ShapeDtypeStruct((bsz, rows, cols), dt)
    outs = pl.pallas_call(
        _dilated_kernel,
        out_shape=[shp(s, B_OUT, BF16), shp(s, B_OUT, F32),
                   shp(s1, d1 * B_OUT, BF16), shp(s1, d1 * B_OUT, F32),
                   shp(s2, d2 * B_OUT, BF16), shp(s2, d2 * B_OUT, F32)],
        grid=(bsz, nblk),
        in_specs=[g0_q, g0_kv, g0_kv, g1_q, g1_kv, g1_kv, g2, g2, g2],
        out_specs=[o0, o0, o1, o1, o2, o2],
        compiler_params=_params("arbitrary", "arbitrary"),
        name="dilated_attn",
    )(qd, kd, vd, v1(qd), v1(kd), v1(vd), v2(qd), v2(kd), v2(vd))
    return [o.reshape(bsz, s, B_OUT) for o in outs]


def _merge_kernel(alpha, x_ref, oa_ref, o0_ref, l0_ref, o1_ref, l1_ref, o2_ref, l2_ref, gt_ref,
                  g1_ref, sc2_ref, sh2_ref, wpa_ref, wpb_ref, wo_ref, lng_ref, lnb_ref, wrt_ref, rb_ref,
                  x1_ref, h2_ref, comb_ref):
    l0, l1, l2 = l0_ref[...], l1_ref[...], l2_ref[...]
    mx = jnp.maximum(jnp.maximum(l0, l1), l2)
    e0, e1, e2 = jnp.exp(l0 - mx), jnp.exp(l1 - mx), jnp.exp(l2 - mx)
    ob = (e0 * o0_ref[...].astype(F32) + e1 * o1_ref[...].astype(F32) + e2 * o2_ref[...].astype(F32)) / (e0 + e1 + e2)
    pa = jnp.dot(oa_ref[...], wpa_ref[...], preferred_element_type=F32)
    pb = jnp.dot(ob.astype(BF16), wpb_ref[...], preferred_element_type=F32)
    merged = gt_ref[:, :D_MODEL].astype(F32) * pa + gt_ref[:, D_MODEL:].astype(F32) * pb
    mix = jnp.dot(merged.astype(BF16), wo_ref[...], preferred_element_type=F32)
    x1 = _ln(alpha * x_ref[...] + g1_ref[...] * mix) * lng_ref[...] + lnb_ref[...]
    x1_ref[...] = x1
    h2 = _ln(x1) * (1.0 + sc2_ref[...]) + sh2_ref[...]
    h2_ref[...] = h2.astype(BF16)

    logits = _nt_dot(wrt_ref[...], h2, precision=lax.Precision.HIGHEST)
    scores = 1.0 / (1.0 + jnp.exp(-logits))
    sel = scores + rb_ref[...]
    sel_r = [sel[e:e + 1, :] for e in range(N_EXPERTS)]
    sc_r = [scores[e:e + 1, :] for e in range(N_EXPERTS)]
    epg = EXPERTS_PER_GROUP
    grp = []
    for g in range(N_GROUPS):
        a = sel_r[g * epg:(g + 1) * epg]
        top2 = None
        for i in range(epg):
            for k in range(i + 1, epg):
                pair = a[i] + a[k]
                top2 = pair if top2 is None else jnp.maximum(top2, pair)
        grp.append(top2)
    best_val = grp[0]
    best = jnp.zeros_like(best_val, dtype=jnp.int32)
    for g in range(1, N_GROUPS):
        upd = grp[g] > best_val
        best = jnp.where(upd, g, best)
        best_val = jnp.where(upd, grp[g], best_val)
    picked = []
    for e in range(N_EXPERTS):
        g = e // epg
        rank = jnp.zeros_like(best)
        for k in range(g * epg, (g + 1) * epg):
            if k < e:
                rank = rank + (sel_r[k] >= sel_r[e]).astype(jnp.int32)
            elif k > e:
                rank = rank + (sel_r[k] > sel_r[e]).astype(jnp.int32)
        chosen = jnp.logical_and(rank < 2, best == g)
        picked.append(jnp.where(chosen, sc_r[e], 0.0))
    tot = picked[0]
    for e in range(1, N_EXPERTS):
        tot = tot + picked[e]
    comb_ref[...] = jnp.concatenate(picked, axis=0) / tot


def _merge_call(alpha, x, oa, dil, gt, g1, sc2, sh2, wpa, wpb, wo, lng, lnb, wrt, rb, l, tm):
    bsz, s, d = x.shape
    nt = s // tm
    row = lambda n: pl.BlockSpec((None, tm, n), lambda b, i: (b, i, 0))
    vec = lambda n: pl.BlockSpec((None, 1, n), lambda b, i: (b, 0, 0))
    lvec = lambda n: pl.BlockSpec((None, 1, n), lambda b, i: (l, 0, 0))
    lmat = lambda r, c: pl.BlockSpec((None, r, c), lambda b, i: (l, 0, 0))
    return pl.pallas_call(
        functools.partial(_merge_kernel, alpha),
        out_shape=[jax.ShapeDtypeStruct((bsz, s, d), F32), jax.ShapeDtypeStruct((bsz, s, d), BF16),
                   jax.ShapeDtypeStruct((N_EXPERTS, bsz * s), F32)],
        grid=(bsz, nt),
        in_specs=[row(d), row(A_Q)] + [row(B_OUT)] * 6 + [row(2 * d), vec(d), vec(d), vec(d),
                  lmat(A_Q, d), lmat(B_OUT, d), lmat(d, d), lvec(d), lvec(d),
                  pl.BlockSpec((N_EXPERTS, d), lambda b, i: (0, 0)),
                  pl.BlockSpec((N_EXPERTS, 1), lambda b, i: (0, 0))],
        out_specs=[row(d), row(d), pl.BlockSpec((N_EXPERTS, tm), lambda b, i: (0, b * nt + i))],
        compiler_params=_params("arbitrary", "arbitrary"),
        name="merge_router",
    )(x, oa, *dil, gt, g1, sc2, sh2, wpa, wpb, wo, lng, lnb, wrt, rb)


def _moe_kernel(alpha, x1_ref, h2_ref, c_ref, wg_ref, wu_ref, wd_ref, g2_ref, lng_ref, lnb_ref, o_ref, acc_ref):
    g = pl.program_id(2)

    @pl.when(g == 0)
    def _():
        acc_ref[...] = jnp.zeros_like(acc_ref)

    h = h2_ref[...]
    c = c_ref[...]
    acc = acc_ref[...]
    for e in range(EXPERTS_PER_GROUP):
        hg = jnp.dot(h, wg_ref[e], preferred_element_type=F32)
        hu = jnp.dot(h, wu_ref[e], preferred_element_type=F32)
        act = hg / (1.0 + jnp.exp(-hg)) * hu * c[:, e:e + 1]
        acc = acc + jnp.dot(act.astype(BF16), wd_ref[e], preferred_element_type=F32)
    acc_ref[...] = acc

    @pl.when(g == N_GROUPS - 1)
    def _():
        o_ref[...] = _ln(alpha * x1_ref[...] + g2_ref[...] * acc) * lng_ref[...] + lnb_ref[...]


def _moe_call(alpha, x1, h2, comb, wg, wu, wd, g2, lng, lnb, l, tm):
    bsz, s, d = x1.shape
    nt = s // tm
    epg = EXPERTS_PER_GROUP
    row = lambda n: pl.BlockSpec((None, tm, n), lambda b, i, g: (b, i, 0))
    lvec = lambda n: pl.BlockSpec((None, 1, n), lambda b, i, g: (l, 0, 0))
    return pl.pallas_call(
        functools.partial(_moe_kernel, alpha),
        out_shape=jax.ShapeDtypeStruct((bsz, s, d), F32),
        grid=(bsz, nt, N_GROUPS),
        in_specs=[row(d), row(d),
                  pl.BlockSpec((None, tm, epg), lambda b, i, g: (g, b * nt + i, 0)),
                  pl.BlockSpec((None, epg, d, D_EXPERT), lambda b, i, g: (l, g, 0, 0)),
                  pl.BlockSpec((None, epg, d, D_EXPERT), lambda b, i, g: (l, g, 0, 0)),
                  pl.BlockSpec((None, epg, D_EXPERT, d), lambda b, i, g: (l, g, 0, 0)),
                  pl.BlockSpec((None, 1, d), lambda b, i, g: (b, 0, 0)),
                  lvec(d), lvec(d)],
        out_specs=row(d),
        scratch_shapes=[pltpu.VMEM((tm, d), F32)],
        compiler_params=_params("arbitrary", "arbitrary", "arbitrary"),
        name="moe",
    )(x1, h2, comb, wg, wu, wd, g2, lng, lnb)


def _rope_tables(seq):
    rows = seq // GRID_W
    row = jnp.repeat(jnp.arange(rows), GRID_W).astype(F32)
    col = jnp.tile(jnp.arange(GRID_W), rows).astype(F32)
    axis_dim = HEAD_DIM // 2
    inv = ROPE_THETA ** (-jnp.arange(0, axis_dim, 2, dtype=F32) / axis_dim)
    ang = jnp.concatenate([row[:, None] * inv, col[:, None] * inv], -1)
    cos = jnp.repeat(jnp.cos(ang), 2, axis=-1)
    sin = jnp.repeat(jnp.sin(ang), 2, axis=-1) * jnp.tile(jnp.array([-1.0, 1.0], F32), HEAD_DIM // 2)
    reps = LANES // HEAD_DIM
    return jnp.tile(cos, (1, reps)), jnp.tile(sin, (1, reps))


def _widen_w_in(w_in):
    depth, d, _ = w_in.shape
    qa = w_in[:, :, :A_Q]
    ka = w_in[:, :, A_Q:A_Q + A_KV].reshape(depth, d, A_KV_HEADS, 1, HEAD_DIM)
    va = w_in[:, :, A_Q + A_KV:A_Q + 2 * A_KV].reshape(depth, d, A_KV_HEADS, 1, HEAD_DIM)
    dup = lambda a: jnp.broadcast_to(a, (depth, d, A_KV_HEADS, 2, HEAD_DIM)).reshape(depth, d, KDUP)
    return jnp.concatenate([qa, dup(ka), dup(va), w_in[:, :, A_Q + 2 * A_KV:]], -1).astype(BF16)


@jax.jit
def _forward(x, c, w_ada, b_ada, w_in, q_norm_g, k_norm_g, w_branch_a, w_branch_b, w_out,
             ln1_g, ln1_b, w_router, router_bias, w_exp_gate, w_exp_up, w_exp_down, ln2_g, ln2_b):
    bsz, s, d = x.shape
    depth = w_in.shape[0]
    alpha = (2.0 * depth) ** 0.25
    cos_t, sin_t = _rope_tables(s)
    seg = jnp.arange(LANES) // HEAD_DIM
    bd = (seg[:, None] == seg[None, :]).astype(BF16)
    reps = LANES // HEAD_DIM
    gq = jnp.tile(q_norm_g * (HEAD_DIM ** -0.5), (1, reps)).reshape(depth, 1, LANES)
    gk = jnp.tile(k_norm_g, (1, reps)).reshape(depth, 1, LANES)
    w_ext = _widen_w_in(w_in)
    wpa, wpb, wo = w_branch_a.astype(BF16), w_branch_b.astype(BF16), w_out.astype(BF16)
    wg, wu, wd = w_exp_gate.astype(BF16), w_exp_up.astype(BF16), w_exp_down.astype(BF16)
    wrt = w_router.T
    rb = router_bias.reshape(N_EXPERTS, 1)
    vec3 = lambda a: a.reshape(depth, 1, d)
    ln1g, ln1b, ln2g, ln2b = vec3(ln1_g), vec3(ln1_b), vec3(ln2_g), vec3(ln2_b)

    mod = _ada_call(c, w_ada, b_ada).reshape(depth, bsz, 6, 1, d)
    for l in range(depth):
        sh1, sc1, g1, sh2, sc2, g2 = [mod[l, :, i] for i in range(6)]
        qa, ka, va, qd, kd, vd, gt = _inproj_call(x, sc1, sh1, w_ext, l, gq, gk, cos_t, sin_t, bd, tm=512)
        oa = _gqa_call(qa, ka, va, tq=512)
        dil = _dilated_call(qd, kd, vd)
        x1, h2, comb_t = _merge_call(alpha, x, oa, dil, gt, g1, sc2, sh2, wpa, wpb, wo, ln1g, ln1b,
                                     wrt, rb, l, tm=512)
        comb = comb_t.reshape(N_GROUPS, EXPERTS_PER_GROUP, bsz * s).transpose(0, 2, 1)
        x = _moe_call(alpha, x1, h2, comb, wg, wu, wd, g2, ln2g, ln2b, l, tm=512)
    return x


def kernel(x, c, w_ada, b_ada, w_in, q_norm_g, k_norm_g, w_branch_a, w_branch_b, w_out, ln1_g, ln1_b,
           w_router, router_bias, w_exp_gate, w_exp_up, w_exp_down, ln2_g, ln2_b):
    return _forward(x, c, w_ada, b_ada, w_in, q_norm_g, k_norm_g, w_branch_a, w_branch_b, w_out, ln1_g, ln1_b,
                    w_router, router_bias, w_exp_gate, w_exp_up, w_exp_down, ln2_g, ln2_b)
```
